```python
import math
import jax, jax.numpy as jnp
from jax import lax
import numpy as np

D_MODEL = 1024
BATCH = 8
SEQ = 4096
DEPTH = 2

N_MIXERS = 2
N_META = 16
N_S5_LAYERS = (DEPTH + N_MIXERS - 1) // N_MIXERS
N_ATTN_LAYERS = DEPTH // N_MIXERS
S5_GROUP = 16
S5_GROUPS = D_MODEL // S5_GROUP
S5_STATE = 64
DT_MIN = 1e-3
DT_MAX = 1e-1
DA_HEADS = 8
DA_HEAD_DIM = D_MODEL // DA_HEADS // 2
DA_V_DIM = 2 * DA_HEAD_DIM
ROPE_THETA = 10000.0
Q_BLOCK = 128
D_FF = 256 * ((int(8 * D_MODEL / 3) + 255) // 256)
CONV_WIDTH = 3
EPS = 1e-6

kernel_name = "hybrid_s5_diffattn_convffn"


def rms_norm(x, gain):
    xf = x.astype(jnp.float32)
    y = xf * lax.rsqrt(jnp.mean(xf * xf, axis=-1, keepdims=True) + EPS)
    return (y * gain.astype(jnp.float32)).astype(x.dtype)


def _complex_linear_combine(left, right):
    a1r, a1i, b1r, b1i = left
    a2r, a2i, b2r, b2i = right
    ar = a2r * a1r - a2i * a1i
    ai = a2r * a1i + a2i * a1r
    br = a2r * b1r - a2i * b1i + b2r
    bi = a2r * b1i + a2i * b1r + b2i
    return ar, ai, br, bi


def s5_mixer(u, a_re, a_im, log_dt, b_re, b_im, c_re, c_im, d_skip, w_glu, b_glu):
    bsz, L, _ = u.shape
    f32 = jnp.float32
    dt = jnp.exp(log_dt.astype(f32))[:, None]
    lr = a_re.astype(f32)
    li = a_im.astype(f32)
    mag = jnp.exp(lr * dt)
    ab_re = mag * jnp.cos(li * dt)
    ab_im = mag * jnp.sin(li * dt)
    den = lr * lr + li * li
    n_re = ab_re - 1.0
    n_im = ab_im
    f_re = ((n_re * lr + n_im * li) / den)[..., None]
    f_im = ((n_im * lr - n_re * li) / den)[..., None]
    br = b_re.astype(f32)
    bi = b_im.astype(f32)
    bb_re = f_re * br - f_im * bi
    bb_im = f_re * bi + f_im * br
    ug = u.astype(f32).reshape(bsz, L, S5_GROUPS, S5_GROUP)
    bu_re = jnp.einsum('blgc,gnc->blgn', ug, bb_re)
    bu_im = jnp.einsum('blgc,gnc->blgn', ug, bb_im)
    a_seq_re = jnp.broadcast_to(ab_re, (1, L, S5_GROUPS, S5_STATE))
    a_seq_im = jnp.broadcast_to(ab_im, (1, L, S5_GROUPS, S5_STATE))
    _, _, s_re, s_im = lax.associative_scan(
        _complex_linear_combine, (a_seq_re, a_seq_im, bu_re, bu_im), axis=1)
    y = (jnp.einsum('blgn,gcn->blgc', s_re, c_re.astype(f32))
         - jnp.einsum('blgn,gcn->blgc', s_im, c_im.astype(f32)))
    y = y.reshape(bsz, L, D_MODEL) + d_skip.astype(f32) * u.astype(f32)
    z = jax.nn.gelu(y).astype(u.dtype)
    h = z @ w_glu + b_glu
    val, gate = jnp.split(h, 2, axis=-1)
    return val * jax.nn.sigmoid(gate)


def rope_tables(L):
    half = DA_HEAD_DIM // 2
    inv_freq = ROPE_THETA ** (-jnp.arange(half, dtype=jnp.float32) / half)
    ang = jnp.arange(L, dtype=jnp.float32)[:, None] * inv_freq[None, :]
    return jnp.cos(ang), jnp.sin(ang)


def apply_rope(x, cos, sin):
    x1, x2 = jnp.split(x.astype(jnp.float32), 2, axis=-1)
    c = cos[None, :, None, None, :]
    s = sin[None, :, None, None, :]
    return jnp.concatenate([x1 * c - x2 * s, x2 * c + x1 * s], axis=-1).astype(x.dtype)


def diff_attention(h, w_qkv, lam, subln_gain, w_o, lambda_init):
    bsz, L, _ = h.shape
    qkv = h @ w_qkv
    q, k, v = jnp.split(qkv, 3, axis=-1)
    q = q.reshape(bsz, L, DA_HEADS, 2, DA_HEAD_DIM)
    k = k.reshape(bsz, L, DA_HEADS, 2, DA_HEAD_DIM)
    v = v.reshape(bsz, L, DA_HEADS, DA_V_DIM)
    cos, sin = rope_tables(L)
    q = apply_rope(q, cos, sin)
    k = apply_rope(k, cos, sin)
    lamf = lam.astype(jnp.float32)
    lam_full = (jnp.exp(jnp.sum(lamf[0] * lamf[1])) - jnp.exp(jnp.sum(lamf[2] * lamf[3]))
                + lambda_init)
    n_blocks = -(-L // Q_BLOCK)
    L_pad = n_blocks * Q_BLOCK
    pad = L_pad - L
    q = jnp.pad(q, ((0, 0), (0, pad), (0, 0), (0, 0), (0, 0)))
    k = jnp.pad(k, ((0, 0), (0, pad), (0, 0), (0, 0), (0, 0)))
    v = jnp.pad(v, ((0, 0), (0, pad), (0, 0), (0, 0)))
    q_blocks = q.reshape(bsz, n_blocks, Q_BLOCK, DA_HEADS, 2, DA_HEAD_DIM).transpose(1, 0, 2, 3, 4, 5)
    key_pos = jnp.arange(L_pad)
    scale = DA_HEAD_DIM ** -0.5

    def one_block(args):
        qb, blk = args
        s = jnp.einsum('bqhcd,bkhcd->bhcqk', qb, k).astype(jnp.float32) * scale
        q_pos = blk * Q_BLOCK + jnp.arange(Q_BLOCK)
        mask = key_pos[None, :] <= q_pos[:, None]
        s = jnp.where(mask, s, -jnp.inf)
        p = jax.nn.softmax(s, axis=-1)
        attn = p[:, :, 0] - lam_full * p[:, :, 1]
        return jnp.einsum('bhqk,bkhe->bqhe', attn.astype(v.dtype), v)

    o = lax.map(one_block, (q_blocks, jnp.arange(n_blocks)))
    o = o.transpose(1, 0, 2, 3, 4).reshape(bsz, L_pad, DA_HEADS, DA_V_DIM)[:, :L]
    o = rms_norm(o, subln_gain) * (1.0 - lambda_init)
    return o.reshape(bsz, L, D_MODEL) @ w_o


def conv_ffn(h, w_up, conv_w, conv_b, w_down):
    L = h.shape[1]
    up = h @ w_up
    up_p = jnp.pad(up, ((0, 0), (CONV_WIDTH - 1, 0), (0, 0)))
    c = conv_b + sum(conv_w[j] * up_p[:, j:j + L] for j in range(CONV_WIDTH))
    gate, val = jnp.split(c, 2, axis=-1)
    return (jax.nn.silu(gate) * val) @ w_down


def setup_inputs(seed: int = 0) -> dict:
    key = jax.random.key(seed)
    ks = jax.random.split(key, 24)
    nrm = jax.random.normal
    G, N, C, D, F = S5_GROUPS, S5_STATE, S5_GROUP, D_MODEL, D_FF
    ns, na = N_S5_LAYERS, N_ATTN_LAYERS
    x = nrm(ks[0], (BATCH, SEQ, D), jnp.float32)
    meta_tokens = nrm(ks[1], (N_META, D), jnp.float32)
    norm_mix_gain = 1.0 + 0.02 * nrm(ks[2], (DEPTH, D), jnp.float32)
    norm_ffn_gain = 1.0 + 0.02 * nrm(ks[3], (DEPTH, D), jnp.float32)
    s5_a_re = -0.5 + 0.01 * nrm(ks[4], (ns, G, N), jnp.float32)
    s5_a_im = (jnp.pi * jnp.arange(N, dtype=jnp.float32))[None, None, :] + 0.01 * nrm(ks[5], (ns, G, N), jnp.float32)
    s5_log_dt = jax.random.uniform(ks[6], (ns, G), jnp.float32, math.log(DT_MIN), math.log(DT_MAX))
    s5_b_re = nrm(ks[7], (ns, G, N, C), jnp.float32) * (2 * C) ** -0.5
    s5_b_im = nrm(ks[8], (ns, G, N, C), jnp.float32) * (2 * C) ** -0.5
    s5_c_re = nrm(ks[9], (ns, G, C, N), jnp.float32) * N ** -0.5
    s5_c_im = nrm(ks[10], (ns, G, C, N), jnp.float32) * N ** -0.5
    s5_d = nrm(ks[11], (ns, D), jnp.float32)
    s5_w_glu = nrm(ks[12], (ns, D, 2 * D), jnp.float32) * D ** -0.5
    s5_b_glu = 0.01 * nrm(ks[13], (ns, 2 * D), jnp.float32)
    attn_w_qkv = nrm(ks[14], (na, D, 3 * D), jnp.float32) * D ** -0.5
    attn_lambda = 0.1 * nrm(ks[15], (na, 4, DA_HEAD_DIM), jnp.float32)
    attn_subln_gain = 1.0 + 0.02 * nrm(ks[16], (na, DA_V_DIM), jnp.float32)
    attn_w_o = nrm(ks[17], (na, D, D), jnp.float32) * D ** -0.5
    ffn_w_up = nrm(ks[18], (DEPTH, D, 2 * F), jnp.float32) * D ** -0.5
    ffn_conv_w = nrm(ks[19], (DEPTH, CONV_WIDTH, 2 * F), jnp.float32) * CONV_WIDTH ** -0.5
    ffn_conv_b = 0.01 * nrm(ks[20], (DEPTH, 2 * F), jnp.float32)
    ffn_w_down = nrm(ks[21], (DEPTH, F, D), jnp.float32) * F ** -0.5
    final_norm_gain = 1.0 + 0.02 * nrm(ks[22], (D,), jnp.float32)
    return {"x": x, "meta_tokens": meta_tokens, "norm_mix_gain": norm_mix_gain,
            "norm_ffn_gain": norm_ffn_gain, "s5_a_re": s5_a_re, "s5_a_im": s5_a_im,
            "s5_log_dt": s5_log_dt, "s5_b_re": s5_b_re, "s5_b_im": s5_b_im,
            "s5_c_re": s5_c_re, "s5_c_im": s5_c_im, "s5_d": s5_d, "s5_w_glu": s5_w_glu,
            "s5_b_glu": s5_b_glu, "attn_w_qkv": attn_w_qkv, "attn_lambda": attn_lambda,
            "attn_subln_gain": attn_subln_gain, "attn_w_o": attn_w_o, "ffn_w_up": ffn_w_up,
            "ffn_conv_w": ffn_conv_w, "ffn_conv_b": ffn_conv_b, "ffn_w_down": ffn_w_down,
            "final_norm_gain": final_norm_gain}


def reference(x, meta_tokens, norm_mix_gain, norm_ffn_gain, s5_a_re, s5_a_im, s5_log_dt,
              s5_b_re, s5_b_im, s5_c_re, s5_c_im, s5_d, s5_w_glu, s5_b_glu, attn_w_qkv,
              attn_lambda, attn_subln_gain, attn_w_o, ffn_w_up, ffn_conv_w, ffn_conv_b,
              ffn_w_down, final_norm_gain):
    bsz = x.shape[0]
    meta = jnp.broadcast_to(meta_tokens.astype(x.dtype)[None], (bsz, N_META, D_MODEL))
    h = jnp.concatenate([meta, x], axis=1)
    for i in range(DEPTH):
        j = i // N_MIXERS
        hn = rms_norm(h, norm_mix_gain[i])
        if i % N_MIXERS == 0:
            mix = s5_mixer(hn, s5_a_re[j], s5_a_im[j], s5_log_dt[j], s5_b_re[j], s5_b_im[j],
                           s5_c_re[j], s5_c_im[j], s5_d[j], s5_w_glu[j], s5_b_glu[j])
        else:
            lambda_init = 0.8 - 0.6 * math.exp(-0.3 * i)
            mix = diff_attention(hn, attn_w_qkv[j], attn_lambda[j], attn_subln_gain[j],
                                 attn_w_o[j], lambda_init)
        h = h + mix.astype(h.dtype)
        hn = rms_norm(h, norm_ffn_gain[i])
        h = h + conv_ffn(hn, ffn_w_up[i], ffn_conv_w[i], ffn_conv_b[i], ffn_w_down[i]).astype(h.dtype)
    out = rms_norm(h, final_norm_gain)
    return out[:, N_META:]
```

```python
import functools
import math

import jax
import jax.numpy as jnp
from jax import lax
from jax.experimental import pallas as pl
from jax.experimental.pallas import tpu as pltpu

F32 = jnp.float32
BF16 = jnp.bfloat16

D_MODEL = 1024
BATCH = 8
SEQ = 4096
N_META = 16
L_REAL = N_META + SEQ
LP = 4224
S5_GROUP = 16
S5_GROUPS = 64
S5_STATE = 64
DT_MIN = 1e-3
DT_MAX = 1e-1
DA_HEADS = 8
DA_HEAD_DIM = 64
DA_V_DIM = 128
ROPE_THETA = 10000.0
D_FF = 2816
CONV_WIDTH = 3
EPS = 1e-6

LANES = 128
SUBLANES = 8
VMEM_LIMIT = 56 * 1024 * 1024

S5_TQ = 64
S5_ROWS = S5_TQ * BATCH
S5_KB = D_MODEL // LANES
S5_CH_STATES = (LANES // S5_GROUP) * S5_STATE

FFN_TM = 528
FFN_FC = 256
FFN_NC = D_FF // FFN_FC
CARRY_ROWS = 8

ATT_T = 384
ATT_NT = LP // ATT_T
PROJ_TM = 1056


def _rmsnorm(x, gain):
    ms = jnp.mean(x * x, axis=-1, keepdims=True)
    return x * lax.rsqrt(ms + EPS) * gain


def _gelu_tanh(y):
    c = math.sqrt(2.0 / math.pi)
    return 0.5 * y * (1.0 + jnp.tanh(c * (y + 0.044715 * (y * y * y))))


def _sigmoid(x):
    return 1.0 / (1.0 + jnp.exp(-x))


def _const_spec(shape):
    nd = len(shape)
    return pl.BlockSpec(shape, lambda *_: (0,) * nd, pipeline_mode=pl.Buffered(1))


def _s5_kernel(h_ref, gain_ref, bblk_ref, abar_ref, cblk_ref, dskip_ref, wglu_ref, bglu_ref,
               o_ref, hn_tm, bu_scr, s_scr, z_scr, mix_tm, st_ref):
    tq = S5_TQ

    @pl.when(pl.program_id(0) == 0)
    def _():
        st_ref[...] = jnp.zeros_like(st_ref)

    gain = gain_ref[...]
    for b in range(BATCH):
        hn = _rmsnorm(h_ref[b], gain)
        for j in range(S5_KB):
            hn_tm[j, pl.ds(b, tq, stride=BATCH), :] = hn[:, LANES * j:LANES * (j + 1)]

    half = S5_CH_STATES
    for kb in range(S5_KB):
        slot = kb % 2
        bu_scr[slot] = jnp.dot(hn_tm[kb].astype(BF16), bblk_ref[kb],
                               preferred_element_type=F32)
        a_re = jnp.broadcast_to(abar_ref[kb, 0:1, :], (SUBLANES, half))
        a_im = jnp.broadcast_to(abar_ref[kb, 1:2, :], (SUBLANES, half))

        def body(t2, carry, slot=slot, a_re=a_re, a_im=a_im):
            s_re, s_im = carry
            base = pl.multiple_of(t2 * (2 * SUBLANES), 2 * SUBLANES)
            bu = bu_scr[slot, pl.ds(base, 2 * SUBLANES), :]
            outs = []
            for u in range(2):
                b_re = bu[SUBLANES * u:SUBLANES * (u + 1), :half]
                b_im = bu[SUBLANES * u:SUBLANES * (u + 1), half:]
                n_re = a_re * s_re - a_im * s_im + b_re
                n_im = a_re * s_im + a_im * s_re + b_im
                s_re, s_im = n_re, n_im
                outs.append(jnp.concatenate([n_re, n_im], axis=1))
            s_scr[pl.ds(base, 2 * SUBLANES), :] = jnp.concatenate(outs, axis=0).astype(BF16)
            return s_re, s_im

        s_re, s_im = lax.fori_loop(0, tq // 2, body, (st_ref[kb, 0], st_ref[kb, 1]))
        st_ref[kb, 0] = s_re
        st_ref[kb, 1] = s_im
        y = jnp.dot(s_scr[...], cblk_ref[kb], preferred_element_type=F32)
        y = y + dskip_ref[:, LANES * kb:LANES * (kb + 1)] * hn_tm[kb]
        z_scr[:, LANES * kb:LANES * (kb + 1)] = _gelu_tanh(y).astype(BF16)

    z = z_scr[...]
    nchunk = 256
    for nb in range(D_MODEL // nchunk):
        c0 = nb * nchunk
        val = jnp.dot(z, wglu_ref[:, c0:c0 + nchunk], preferred_element_type=F32)
        val = val + bglu_ref[:, c0:c0 + nchunk]
        gate = jnp.dot(z, wglu_ref[:, D_MODEL + c0:D_MODEL + c0 + nchunk],
                       preferred_element_type=F32)
        gate = gate + bglu_ref[:, D_MODEL + c0:D_MODEL + c0 + nchunk]
        mix = val * _sigmoid(gate)
        for jj in range(nchunk // LANES):
            mix_tm[nb * (nchunk // LANES) + jj] = mix[:, LANES * jj:LANES * (jj + 1)]

    for b in range(BATCH):
        for j in range(S5_KB):
            sl = slice(LANES * j, LANES * (j + 1))
            o_ref[b, :, sl] = h_ref[b, :, sl] + mix_tm[j, pl.ds(b, tq, stride=BATCH), :]


def _s5_params(a_re, a_im, log_dt, b_re, b_im, c_re, c_im):
    dt = jnp.exp(log_dt.astype(F32))[:, None]
    lr = a_re.astype(F32)
    li = a_im.astype(F32)
    mag = jnp.exp(lr * dt)
    ab_re = mag * jnp.cos(li * dt)
    ab_im = mag * jnp.sin(li * dt)
    den = lr * lr + li * li
    n_re = ab_re - 1.0
    n_im = ab_im
    f_re = ((n_re * lr + n_im * li) / den)[..., None]
    f_im = ((n_im * lr - n_re * li) / den)[..., None]
    br = b_re.astype(F32)
    bi = b_im.astype(F32)
    bb_re = f_re * br - f_im * bi
    bb_im = f_re * bi + f_im * br
    gpb = LANES // S5_GROUP
    eye = jnp.eye(gpb, dtype=F32)

    def pack_b(bb):
        t = bb.reshape(S5_KB, gpb, S5_STATE, S5_GROUP)
        t = jnp.einsum('kgnc,gh->kgchn', t, eye)
        return t.reshape(S5_KB, LANES, S5_CH_STATES)

    bblk = jnp.concatenate([pack_b(bb_re), pack_b(bb_im)], axis=-1).astype(BF16)

    def pack_c(cc):
        t = cc.astype(F32).reshape(S5_KB, gpb, S5_GROUP, S5_STATE)
        t = jnp.einsum('kgcn,gh->kgnhc', t, eye)
        return t.reshape(S5_KB, S5_CH_STATES, LANES)

    cblk = jnp.concatenate([pack_c(c_re), -pack_c(c_im)], axis=1).astype(BF16)
    abar = jnp.stack([ab_re.reshape(S5_KB, S5_CH_STATES),
                      ab_im.reshape(S5_KB, S5_CH_STATES)], axis=1)
    return bblk, abar, cblk


def _s5_mixer(h, gain, a_re, a_im, log_dt, b_re, b_im, c_re, c_im, d_skip, w_glu, b_glu):
    bblk, abar, cblk = _s5_params(a_re, a_im, log_dt, b_re, b_im, c_re, c_im)
    blk = pl.BlockSpec((BATCH, S5_TQ, D_MODEL), lambda i: (0, i, 0))
    return pl.pallas_call(
        _s5_kernel,
        out_shape=jax.ShapeDtypeStruct((BATCH, LP, D_MODEL), F32),
        grid=(LP // S5_TQ,),
        in_specs=[
            blk,
            _const_spec((1, D_MODEL)),
            _const_spec((S5_KB, LANES, 2 * S5_CH_STATES)),
            _const_spec((S5_KB, 2, S5_CH_STATES)),
            _const_spec((S5_KB, 2 * S5_CH_STATES, LANES)),
            _const_spec((1, D_MODEL)),
            _const_spec((D_MODEL, 2 * D_MODEL)),
            _const_spec((1, 2 * D_MODEL)),
        ],
        out_specs=blk,
        scratch_shapes=[
            pltpu.VMEM((S5_KB, S5_ROWS, LANES), F32),
            pltpu.VMEM((2, S5_ROWS, 2 * S5_CH_STATES), F32),
            pltpu.VMEM((S5_ROWS, 2 * S5_CH_STATES), BF16),
            pltpu.VMEM((S5_ROWS, D_MODEL), BF16),
            pltpu.VMEM((S5_KB, S5_ROWS, LANES), F32),
            pltpu.VMEM((S5_KB, 2, SUBLANES, S5_CH_STATES), F32),
        ],
        compiler_params=pltpu.CompilerParams(
            dimension_semantics=("arbitrary",), vmem_limit_bytes=VMEM_LIMIT),
        name="s5_mixer",
    )(h, gain.reshape(1, D_MODEL).astype(F32), bblk, abar, cblk,
      d_skip.reshape(1, D_MODEL).astype(F32), w_glu.astype(BF16),
      b_glu.reshape(1, 2 * D_MODEL).astype(F32))


def _ffn_kernel(x_ref, gain_ref, wup_ref, cw_ref, wdn_ref, fgain_ref, o_ref,
                hn_scr, acc_scr, up_scr, carry_scr, *, final_norm):
    tm = FFN_TM
    tile_in_batch = pl.program_id(0) % (LP // tm)

    @pl.when(pl.program_id(0) == 0)
    def _():
        carry_scr[...] = jnp.zeros_like(carry_scr)

    x = x_ref[...]
    hn_scr[...] = _rmsnorm(x, gain_ref[...]).astype(BF16)
    acc_scr[...] = jnp.zeros_like(acc_scr)

    def chunk(c, _):
        up = jnp.dot(hn_scr[...], wup_ref[c], preferred_element_type=F32)
        prev = jnp.where(tile_in_batch == 0, 0.0, carry_scr[c])
        up_scr[0:CARRY_ROWS, :] = prev
        up_scr[CARRY_ROWS:, :] = up
        carry_scr[c] = up[tm - CARRY_ROWS:, :]
        cw = cw_ref[c]
        conv = (cw[3:4, :]
                + cw[0:1, :] * up_scr[CARRY_ROWS - 2:CARRY_ROWS - 2 + tm, :]
                + cw[1:2, :] * up_scr[CARRY_ROWS - 1:CARRY_ROWS - 1 + tm, :]
                + cw[2:3, :] * up)
        g = conv[:, :FFN_FC]
        v = conv[:, FFN_FC:]
        act = (g * _sigmoid(g) * v).astype(BF16)
        acc_scr[...] += jnp.dot(act, wdn_ref[c], preferred_element_type=F32)
        return 0

    lax.fori_loop(0, FFN_NC, chunk, 0)
    y = x + acc_scr[...]
    if final_norm:
        y = _rmsnorm(y, fgain_ref[...])
    o_ref[...] = y


def _conv_ffn(h2d, gain, w_up, conv_w, conv_b, w_down, fgain, final_norm):
    nrows = h2d.shape[0]
    f = D_FF
    wg = w_up[:, :f].reshape(D_MODEL, FFN_NC, FFN_FC)
    wv = w_up[:, f:].reshape(D_MODEL, FFN_NC, FFN_FC)
    wup = jnp.concatenate([wg, wv], axis=-1).transpose(1, 0, 2).astype(BF16)
    taps = jnp.concatenate([conv_w.astype(F32), conv_b.astype(F32)[None]], axis=0)
    tg = taps[:, :f].reshape(4, FFN_NC, FFN_FC)
    tv = taps[:, f:].reshape(4, FFN_NC, FFN_FC)
    cw = jnp.concatenate([tg, tv], axis=-1).transpose(1, 0, 2)
    cw = jnp.pad(cw, ((0, 0), (0, SUBLANES - 4), (0, 0)))
    wdn = w_down.reshape(FFN_NC, FFN_FC, D_MODEL).astype(BF16)
    row_blk = pl.BlockSpec((FFN_TM, D_MODEL), lambda i: (i, 0))
    return pl.pallas_call(
        functools.partial(_ffn_kernel, final_norm=final_norm),
        out_shape=jax.ShapeDtypeStruct((nrows, D_MODEL), F32),
        grid=(nrows // FFN_TM,),
        in_specs=[
            row_blk,
            _const_spec((1, D_MODEL)),
            _const_spec((FFN_NC, D_MODEL, 2 * FFN_FC)),
            _const_spec((FFN_NC, SUBLANES, 2 * FFN_FC)),
            _const_spec((FFN_NC, FFN_FC, D_MODEL)),
            _const_spec((1, D_MODEL)),
        ],
        out_specs=row_blk,
        scratch_shapes=[
            pltpu.VMEM((FFN_TM, D_MODEL), BF16),
            pltpu.VMEM((FFN_TM, D_MODEL), F32),
            pltpu.VMEM((FFN_TM + CARRY_ROWS, 2 * FFN_FC), F32),
            pltpu.VMEM((FFN_NC, CARRY_ROWS, 2 * FFN_FC), F32),
        ],
        compiler_params=pltpu.CompilerParams(
            dimension_semantics=("arbitrary",), vmem_limit_bytes=VMEM_LIMIT),
        name="conv_ffn_final" if final_norm else "conv_ffn",
    )(h2d, gain.reshape(1, D_MODEL).astype(F32), wup, cw, wdn,
      fgain.reshape(1, D_MODEL).astype(F32))


def _rope(x, cos, sin_signed, first_half):
    hd2 = DA_HEAD_DIM // 2
    partner = jnp.where(first_half,
                        pltpu.roll(x, LANES - hd2, axis=1),
                        pltpu.roll(x, hd2, axis=1))
    return x * cos + partner * sin_signed


def _qkv_kernel(h_ref, gain_ref, wq_ref, wk_ref, wvt_ref, cos_ref, sin_ref,
                q_ref, k_ref, vt_ref):
    hn = _rmsnorm(h_ref[0], gain_ref[...]).astype(BF16)
    cos = cos_ref[...]
    sin_signed = sin_ref[...]
    lane = lax.broadcasted_iota(jnp.int32, (ATT_T, LANES), 1)
    first_half = (lane % DA_HEAD_DIM) < (DA_HEAD_DIM // 2)
    scale = DA_HEAD_DIM ** -0.5
    for hh in range(DA_HEADS):
        sl = slice(LANES * hh, LANES * (hh + 1))
        q = jnp.dot(hn, wq_ref[:, sl], preferred_element_type=F32)
        q_ref[0, :, sl] = (_rope(q, cos, sin_signed, first_half) * scale).astype(BF16)
        k = jnp.dot(hn, wk_ref[:, sl], preferred_element_type=F32)
        k_ref[0, :, sl] = _rope(k, cos, sin_signed, first_half).astype(BF16)
    vt = lax.dot_general(wvt_ref[...], hn, (((1,), (1,)), ((), ())),
                         preferred_element_type=F32)
    vt_ref[0, 0] = vt.astype(BF16)


def _rope_tables():
    half = DA_HEAD_DIM // 2
    inv_freq = ROPE_THETA ** (-jnp.arange(half, dtype=F32) / half)
    ang = jnp.arange(LP, dtype=F32)[:, None] * inv_freq[None, :]
    cos = jnp.cos(ang)
    sin = jnp.sin(ang)
    reps = LANES // DA_HEAD_DIM
    cos_t = jnp.tile(jnp.concatenate([cos, cos], axis=1), (1, reps))
    sin_t = jnp.tile(jnp.concatenate([-sin, sin], axis=1), (1, reps))
    return cos_t, sin_t


def _qkv_rope(h, gain, w_qkv):
    cos_t, sin_t = _rope_tables()
    wq = w_qkv[:, :D_MODEL].astype(BF16)
    wk = w_qkv[:, D_MODEL:2 * D_MODEL].astype(BF16)
    wvt = w_qkv[:, 2 * D_MODEL:].T.astype(BF16)
    tok_blk = pl.BlockSpec((1, ATT_T, D_MODEL), lambda b, i: (b, i, 0))
    tab_blk = pl.BlockSpec((ATT_T, LANES), lambda b, i: (i, 0))
    return pl.pallas_call(
        _qkv_kernel,
        out_shape=(jax.ShapeDtypeStruct((BATCH, LP, D_MODEL), BF16),
                   jax.ShapeDtypeStruct((BATCH, LP, D_MODEL), BF16),
                   jax.ShapeDtypeStruct((BATCH, ATT_NT, D_MODEL, ATT_T), BF16)),
        grid=(BATCH, ATT_NT),
        in_specs=[
            tok_blk,
            _const_spec((1, D_MODEL)),
            _const_spec((D_MODEL, D_MODEL)),
            _const_spec((D_MODEL, D_MODEL)),
            _const_spec((D_MODEL, D_MODEL)),
            tab_blk,
            tab_blk,
        ],
        out_specs=(tok_blk, tok_blk,
                   pl.BlockSpec((1, 1, D_MODEL, ATT_T), lambda b, i: (b, i, 0, 0))),
        compiler_params=pltpu.CompilerParams(
            dimension_semantics=("arbitrary", "arbitrary"), vmem_limit_bytes=VMEM_LIMIT),
        name="qkv_rope",
    )(h, gain.reshape(1, D_MODEL).astype(F32), wq, wk, wvt, cos_t, sin_t)


def _attn_kernel(lam_ref, q_ref, k_ref, vt_ref, sgain_ref, o_ref,
                 qbd_scr, m_scr, l_scr, acc_scr, *, out_scale):
    t = ATT_T
    qi = pl.program_id(2)
    q = q_ref[0]
    lane = lax.broadcasted_iota(jnp.int32, (t, LANES), 1)
    zero = jnp.zeros_like(q)
    qbd_scr[0:t, :] = jnp.where(lane < DA_HEAD_DIM, q, zero)
    qbd_scr[t:, :] = jnp.where(lane >= DA_HEAD_DIM, q, zero)
    m_scr[...] = jnp.full_like(m_scr, -jnp.inf)
    l_scr[...] = jnp.zeros_like(l_scr)
    acc_scr[...] = jnp.zeros_like(acc_scr)

    def step(kj, masked):
        start = pl.multiple_of(kj * t, t)
        k_t = k_ref[0, pl.ds(start, t), :]
        s = lax.dot_general(k_t, qbd_scr[...], (((1,), (1,)), ((), ())),
                            preferred_element_type=F32)
        if masked:
            key_pos = lax.broadcasted_iota(jnp.int32, (t, 2 * t), 0)
            col = lax.broadcasted_iota(jnp.int32, (t, 2 * t), 1)
            q_pos = jnp.where(col >= t, col - t, col)
            s = jnp.where(key_pos <= q_pos, s, -jnp.inf)
        m_old = m_scr[...]
        m_new = jnp.maximum(m_old, jnp.max(s, axis=0, keepdims=True))
        alpha = jnp.exp(m_old - m_new)
        p = jnp.exp(s - m_new)
        l_scr[...] = alpha * l_scr[...] + jnp.sum(p, axis=0, keepdims=True)
        pv = jnp.dot(vt_ref[0, kj], p.astype(BF16), preferred_element_type=F32)
        acc_scr[...] = alpha * acc_scr[...] + pv
        m_scr[...] = m_new

    def loop_body(kj, carry):
        step(kj, masked=False)
        return carry

    lax.fori_loop(0, qi, loop_body, 0)
    step(qi, masked=True)

    lam = lam_ref[0, 0]
    o_all = acc_scr[...] / l_scr[...]
    o = o_all[:, :t] - lam * o_all[:, t:]
    ms = jnp.mean(o * o, axis=0, keepdims=True)
    o = o * lax.rsqrt(ms + EPS)
    o_ref[0] = (o.T * sgain_ref[...] * out_scale).astype(BF16)


def _diff_attention(q, k, vt, lam_full, subln_gain, lambda_init):
    return pl.pallas_call(
        functools.partial(_attn_kernel, out_scale=1.0 - lambda_init),
        out_shape=jax.ShapeDtypeStruct((BATCH, LP, D_MODEL), BF16),
        grid=(BATCH, DA_HEADS, ATT_NT),
        in_specs=[
            pl.BlockSpec(memory_space=pltpu.SMEM),
            pl.BlockSpec((1, ATT_T, LANES), lambda b, h, i: (b, i, h)),
            pl.BlockSpec((1, LP, LANES), lambda b, h, i: (b, 0, h)),
            pl.BlockSpec((1, ATT_NT, DA_V_DIM, ATT_T), lambda b, h, i: (b, 0, h, 0)),
            _const_spec((1, DA_V_DIM)),
        ],
        out_specs=pl.BlockSpec((1, ATT_T, LANES), lambda b, h, i: (b, i, h)),
        scratch_shapes=[
            pltpu.VMEM((2 * ATT_T, LANES), BF16),
            pltpu.VMEM((1, 2 * ATT_T), F32),
            pltpu.VMEM((1, 2 * ATT_T), F32),
            pltpu.VMEM((DA_V_DIM, 2 * ATT_T), F32),
        ],
        compiler_params=pltpu.CompilerParams(
            dimension_semantics=("arbitrary", "arbitrary", "arbitrary"),
            vmem_limit_bytes=VMEM_LIMIT),
        name="diff_attn",
    )(lam_full.reshape(1, 1).astype(F32), q, k, vt,
      subln_gain.reshape(1, DA_V_DIM).astype(F32))


def _proj_kernel(x_ref, w_ref, res_ref, o_ref):
    o_ref[...] = res_ref[...] + jnp.dot(x_ref[...], w_ref[...], preferred_element_type=F32)


def _out_proj(x2d, w, res2d):
    nrows = x2d.shape[0]
    row_blk = pl.BlockSpec((PROJ_TM, D_MODEL), lambda i: (i, 0))
    return pl.pallas_call(
        _proj_kernel,
        out_shape=jax.ShapeDtypeStruct((nrows, D_MODEL), F32),
        grid=(nrows // PROJ_TM,),
        in_specs=[row_blk, _const_spec((D_MODEL, D_MODEL)), row_blk],
        out_specs=row_blk,
        compiler_params=pltpu.CompilerParams(
            dimension_semantics=("arbitrary",), vmem_limit_bytes=VMEM_LIMIT),
        name="out_proj",
    )(x2d, w.astype(BF16), res2d)


def kernel(x, meta_tokens, norm_mix_gain, norm_ffn_gain, s5_a_re, s5_a_im, s5_log_dt,
           s5_b_re, s5_b_im, s5_c_re, s5_c_im, s5_d, s5_w_glu, s5_b_glu, attn_w_qkv,
           attn_lambda, attn_subln_gain, attn_w_o, ffn_w_up, ffn_conv_w, ffn_conv_b,
           ffn_w_down, final_norm_gain):
    bsz = x.shape[0]
    meta = jnp.broadcast_to(meta_tokens.astype(F32)[None], (bsz, N_META, D_MODEL))
    pad = jnp.zeros((bsz, LP - L_REAL, D_MODEL), F32)
    h = jnp.concatenate([meta, x.astype(F32), pad], axis=1)

    h = _s5_mixer(h, norm_mix_gain[0], s5_a_re[0], s5_a_im[0], s5_log_dt[0], s5_b_re[0],
                  s5_b_im[0], s5_c_re[0], s5_c_im[0], s5_d[0], s5_w_glu[0], s5_b_glu[0])
    h2d = h.reshape(bsz * LP, D_MODEL)
    h2d = _conv_ffn(h2d, norm_ffn_gain[0], ffn_w_up[0], ffn_conv_w[0], ffn_conv_b[0],
                    ffn_w_down[0], final_norm_gain, final_norm=False)

    lambda_init = 0.8 - 0.6 * math.exp(-0.3 * 1)
    lamf = attn_lambda[0].astype(F32)
    lam_full = (jnp.exp(jnp.sum(lamf[0] * lamf[1])) - jnp.exp(jnp.sum(lamf[2] * lamf[3]))
                + lambda_init)
    q, k, vt = _qkv_rope(h2d.reshape(bsz, LP, D_MODEL), norm_mix_gain[1], attn_w_qkv[0])
    o = _diff_attention(q, k, vt, lam_full, attn_subln_gain[0], lambda_init)
    h2d = _out_proj(o.reshape(bsz * LP, D_MODEL), attn_w_o[0], h2d)
    h2d = _conv_ffn(h2d, norm_ffn_gain[1], ffn_w_up[1], ffn_conv_w[1], ffn_conv_b[1],
                    ffn_w_down[1], final_norm_gain, final_norm=True)
    return h2d.reshape(bsz, LP, D_MODEL)[:, N_META:L_REAL]
```
